```python
import jax, jax.numpy as jnp
from jax import lax
import numpy as np

D_MODEL = 1024
BATCH = 8
SEQ = 4096
DEPTH = 1

CHUNK = 64
MEM_LEN = 256
GLA_HEADS = 4
GLA_DK = D_MODEL // 2
GLA_DV = D_MODEL
GLA_HDK = GLA_DK // GLA_HEADS
GLA_HDV = GLA_DV // GLA_HEADS
GLA_GATE_RANK = 16
GLA_GATE_TEMP = 16.0
POOL_WINDOWS = (2, 4, 8, 16)
POOL_GROUPS = len(POOL_WINDOWS)
POOL_WIDTH = D_MODEL // 2
POOL_GROUP_DIM = POOL_WIDTH // POOL_GROUPS
XA_HEADS = 4
XA_HEAD_DIM = 128
XA_WIDTH = XA_HEADS * XA_HEAD_DIM
N_BRANCH = 3
D_FF = 2816
EPS = 1e-6

IN_SPLITS = (GLA_DK, GLA_DK, GLA_DV, GLA_DV, GLA_GATE_RANK, POOL_WIDTH, XA_WIDTH, N_BRANCH * D_MODEL)
IN_WIDTH = sum(IN_SPLITS)

kernel_name = "hybrid_gla_pool_memxattn_macaron_block"


def rms_norm(x, g):
    xf = x.astype(jnp.float32)
    y = xf * lax.rsqrt(jnp.mean(xf * xf, axis=-1, keepdims=True) + EPS)
    return (y * g.astype(jnp.float32)).astype(x.dtype)


def swiglu(h, w_in, w_out):
    a, b = jnp.split(h @ w_in, 2, axis=-1)
    return (jax.nn.silu(a) * b) @ w_out


def gla_chunked(q, k, v, log_a):
    B, S = q.shape[0], q.shape[1]
    nc = S // CHUNK

    def to_chunks(t):
        return t.reshape(B, nc, CHUNK, GLA_HEADS, t.shape[-1]).transpose(1, 0, 3, 2, 4)

    qc, kc, vc = to_chunks(q), to_chunks(k), to_chunks(v)
    b = jnp.cumsum(to_chunks(log_a.astype(jnp.float32)), axis=3)
    b_end = b[:, :, :, -1:, :]
    kt = (kc.astype(jnp.float32) * jnp.exp(b_end - b)).astype(v.dtype)
    decay = jnp.exp(b_end[:, :, :, 0, :]).astype(v.dtype)

    def step(state, inp):
        q_c, k_c, v_c, d_c = inp
        state = d_c[..., None] * state + jnp.einsum('bhck,bhcv->bhkv', k_c, v_c)
        o = jnp.einsum('bhck,bhkv->bhcv', q_c, state)
        return state, o

    s0 = jnp.zeros((B, GLA_HEADS, GLA_HDK, GLA_HDV), v.dtype)
    _, o = lax.scan(step, s0, (qc, kt, vc, decay))
    return o.transpose(1, 0, 3, 2, 4).reshape(B, S, GLA_DV)


def multiscale_pool(p, w_pool, pool_scale):
    B, S, _ = p.shape
    pg = p.reshape(B, S, POOL_GROUPS, POOL_GROUP_DIM).astype(jnp.float32)
    c0 = jnp.concatenate([jnp.zeros((B, 1, POOL_GROUPS, POOL_GROUP_DIM), jnp.float32),
                          jnp.cumsum(pg, axis=1)], axis=1)
    pos = jnp.arange(1, S + 1, dtype=jnp.float32)
    outs = []
    for g, w in enumerate(POOL_WINDOWS):
        cg = c0[:, :, g]
        lag = jnp.concatenate([jnp.zeros((B, w - 1, POOL_GROUP_DIM), jnp.float32), cg[:, :S + 1 - w]], axis=1)
        cnt = jnp.minimum(pos, float(w))[None, :, None]
        outs.append((cg[:, 1:] - lag) / cnt - pg[:, :, g])
    mixed = jnp.stack(outs, axis=2).astype(p.dtype)
    y = jnp.einsum('bsgc,gcd->bsgd', mixed, w_pool).reshape(B, S, POOL_WIDTH)
    return y * pool_scale


def memory_cross_attention(xq, mem_n, w_mem_kv):
    B, S, _ = xq.shape
    M = mem_n.shape[1]
    q = xq.reshape(B, S, XA_HEADS, XA_HEAD_DIM)
    k, v = jnp.split(mem_n @ w_mem_kv, 2, axis=-1)
    k = k.reshape(B, M, XA_HEADS, XA_HEAD_DIM)
    v = v.reshape(B, M, XA_HEADS, XA_HEAD_DIM)
    s = jnp.einsum('bshd,bmhd->bhsm', q, k).astype(jnp.float32) * (XA_HEAD_DIM ** -0.5)
    pr = jax.nn.softmax(s, axis=-1).astype(v.dtype)
    return jnp.einsum('bhsm,bmhd->bshd', pr, v).reshape(B, S, XA_WIDTH)


def token_mixing(h, mem, w_in, w_fu, b_f, gla_norm_g, w_pool, pool_scale, mem_norm_g, w_mem_kv,
                 w_up_gla, w_up_pool, w_up_xattn, w_o):
    B, S, _ = h.shape
    idx = np.cumsum(IN_SPLITS)[:-1].tolist()
    q, k, v, g_out, f_low, p_in, xq, gates = jnp.split(h @ w_in, idx, axis=-1)
    q = q.reshape(B, S, GLA_HEADS, GLA_HDK) * (GLA_HDK ** -0.5)
    k = k.reshape(B, S, GLA_HEADS, GLA_HDK)
    v = v.reshape(B, S, GLA_HEADS, GLA_HDV)
    f = (f_low @ w_fu + b_f).astype(jnp.float32)
    log_a = (jax.nn.log_sigmoid(f) / GLA_GATE_TEMP).reshape(B, S, GLA_HEADS, GLA_HDK)
    o = gla_chunked(q, k, v, log_a).reshape(B, S, GLA_HEADS, GLA_HDV)
    o = rms_norm(o, gla_norm_g.reshape(GLA_HEADS, GLA_HDV)).reshape(B, S, GLA_DV)
    y_a = (o * jax.nn.silu(g_out)) @ w_up_gla
    y_b = multiscale_pool(p_in, w_pool, pool_scale) @ w_up_pool
    y_c = memory_cross_attention(xq, rms_norm(mem, mem_norm_g), w_mem_kv) @ w_up_xattn
    gt = jax.nn.sigmoid(gates.reshape(B, S, N_BRANCH, D_MODEL))
    merged = gt[:, :, 0] * y_a + gt[:, :, 1] * y_b + gt[:, :, 2] * y_c
    return merged @ w_o


def setup_inputs(seed: int = 0) -> dict:
    key = jax.random.key(seed)
    ks = jax.random.split(key, 32)
    L = DEPTH

    def dense(k, shape, fan_in):
        return jax.random.normal(k, shape, jnp.float32) * (fan_in ** -0.5)

    def gain(k, n):
        return 1.0 + 0.02 * jax.random.normal(k, (L, n), jnp.float32)

    return {
        "x": jax.random.normal(ks[0], (BATCH, SEQ, D_MODEL), jnp.float32),
        "mem": jax.random.normal(ks[1], (BATCH, MEM_LEN, D_MODEL), jnp.float32),
        "ffn1_pre_g": gain(ks[2], D_MODEL),
        "ffn1_w_in": dense(ks[3], (L, D_MODEL, 2 * D_FF), D_MODEL),
        "ffn1_w_out": dense(ks[4], (L, D_FF, D_MODEL), D_FF),
        "ffn1_post_g": gain(ks[5], D_MODEL),
        "mix_pre_g": gain(ks[6], D_MODEL),
        "w_in": dense(ks[7], (L, D_MODEL, IN_WIDTH), D_MODEL),
        "w_fu": dense(ks[8], (L, GLA_GATE_RANK, GLA_DK), GLA_GATE_RANK),
        "b_f": 0.1 * jax.random.normal(ks[9], (L, GLA_DK), jnp.float32),
        "gla_norm_g": gain(ks[10], GLA_DV),
        "w_pool": dense(ks[11], (L, POOL_GROUPS, POOL_GROUP_DIM, POOL_GROUP_DIM), POOL_GROUP_DIM),
        "pool_scale": gain(ks[12], POOL_WIDTH),
        "mem_norm_g": gain(ks[13], D_MODEL),
        "w_mem_kv": dense(ks[14], (L, D_MODEL, 2 * XA_WIDTH), D_MODEL),
        "w_up_gla": dense(ks[15], (L, GLA_DV, D_MODEL), GLA_DV),
        "w_up_pool": dense(ks[16], (L, POOL_WIDTH, D_MODEL), POOL_WIDTH),
        "w_up_xattn": dense(ks[17], (L, XA_WIDTH, D_MODEL), XA_WIDTH),
        "w_o": dense(ks[18], (L, D_MODEL, D_MODEL), D_MODEL),
        "mix_post_g": gain(ks[19], D_MODEL),
        "ffn2_pre_g": gain(ks[20], D_MODEL),
        "ffn2_w_in": dense(ks[21], (L, D_MODEL, 2 * D_FF), D_MODEL),
        "ffn2_w_out": dense(ks[22], (L, D_FF, D_MODEL), D_FF),
        "ffn2_post_g": gain(ks[23], D_MODEL),
        "final_g": gain(ks[24], D_MODEL),
    }


def reference(x, mem, ffn1_pre_g, ffn1_w_in, ffn1_w_out, ffn1_post_g, mix_pre_g, w_in, w_fu, b_f,
              gla_norm_g, w_pool, pool_scale, mem_norm_g, w_mem_kv, w_up_gla, w_up_pool, w_up_xattn,
              w_o, mix_post_g, ffn2_pre_g, ffn2_w_in, ffn2_w_out, ffn2_post_g, final_g):
    for l in range(DEPTH):
        x = x + 0.5 * rms_norm(swiglu(rms_norm(x, ffn1_pre_g[l]), ffn1_w_in[l], ffn1_w_out[l]), ffn1_post_g[l])
        h = rms_norm(x, mix_pre_g[l])
        y = token_mixing(h, mem, w_in[l], w_fu[l], b_f[l], gla_norm_g[l], w_pool[l], pool_scale[l],
                         mem_norm_g[l], w_mem_kv[l], w_up_gla[l], w_up_pool[l], w_up_xattn[l], w_o[l])
        x = x + rms_norm(y, mix_post_g[l])
        x = x + 0.5 * rms_norm(swiglu(rms_norm(x, ffn2_pre_g[l]), ffn2_w_in[l], ffn2_w_out[l]), ffn2_post_g[l])
        x = rms_norm(x, final_g[l])
    return x
```

```python
import functools

import jax
import jax.numpy as jnp
from jax import lax
from jax.experimental import pallas as pl
from jax.experimental.pallas import tpu as pltpu

EPS = 1e-6
CHUNK = 64
GLA_HEADS = 4
GLA_GATE_TEMP = 16.0
POOL_WINDOWS = (2, 4, 8, 16)
POOL_HIST = 16
XA_HEADS = 4
N_BRANCH = 3

LANES = 128
MXU_DIM = 256
VMEM_LIMIT_BYTES = 56 * 1024 * 1024

F32 = jnp.float32
BF16 = jnp.bfloat16


def _rms(x, g):
    return x * lax.rsqrt(jnp.mean(x * x, axis=-1, keepdims=True) + EPS) * g


def _dot(a, b):
    return jnp.dot(a, b, preferred_element_type=F32)


def _resident(shape):
    return pl.BlockSpec(shape, lambda *_: (0,) * len(shape), pipeline_mode=pl.Buffered(1))


def _ffn_chunks(d_ff):
    step = 4 * MXU_DIM
    return [(c0, min(step, d_ff - c0)) for c0 in range(0, d_ff, step)]


def _ffn_kernel(x_ref, pre_g_ref, w_in_ref, w_out_ref, post_g_ref, *rest, d_ff, final_norm):
    if final_norm:
        final_g_ref, o_ref = rest
    else:
        (o_ref,) = rest
    x = x_ref[...]
    h = _rms(x, pre_g_ref[...]).astype(BF16)
    acc = None
    for c0, cn in _ffn_chunks(d_ff):
        a = _dot(h, w_in_ref[:, c0:c0 + cn])
        b = _dot(h, w_in_ref[:, d_ff + c0:d_ff + c0 + cn])
        g = (a * jax.nn.sigmoid(a) * b).astype(BF16)
        p = _dot(g, w_out_ref[c0:c0 + cn, :])
        acc = p if acc is None else acc + p
    y = x + 0.5 * _rms(acc, post_g_ref[...])
    if final_norm:
        y = _rms(y, final_g_ref[...])
    o_ref[...] = y


def _ffn(x2d, pre_g, w_in, w_out, post_g, final_g, *, tm):
    t, d = x2d.shape
    d_ff = w_out.shape[0]
    assert t % tm == 0 and d_ff % MXU_DIM == 0
    final_norm = final_g is not None
    row = pl.BlockSpec((tm, d), lambda i: (i, 0))
    vec = _resident((1, d))
    in_specs = [row, vec, _resident(w_in.shape), _resident(w_out.shape), vec]
    args = [x2d, pre_g, w_in, w_out, post_g]
    if final_norm:
        in_specs.append(vec)
        args.append(final_g)
    return pl.pallas_call(
        functools.partial(_ffn_kernel, d_ff=d_ff, final_norm=final_norm),
        out_shape=jax.ShapeDtypeStruct((t, d), F32),
        grid=(t // tm,),
        in_specs=in_specs,
        out_specs=row,
        compiler_params=pltpu.CompilerParams(
            dimension_semantics=("arbitrary",), vmem_limit_bytes=VMEM_LIMIT_BYTES),
        name="ffn_final" if final_norm else "ffn",
    )(*args)


def _memkv_kernel(mem_ref, g_ref, w_kt_ref, w_v_ref, kt_ref, v_ref):
    mem_n = _rms(mem_ref[...], g_ref[...]).astype(BF16)
    kt = lax.dot_general(w_kt_ref[...], mem_n, (((1,), (1,)), ((), ())),
                         preferred_element_type=F32)
    kt_ref[...] = kt.astype(BF16)
    v_ref[...] = _dot(mem_n, w_v_ref[...]).astype(BF16)


def _memkv(mem, g, w_kt, w_v):
    b, m, d = mem.shape
    xa = w_v.shape[1]
    return pl.pallas_call(
        _memkv_kernel,
        out_shape=(jax.ShapeDtypeStruct((b, xa, m), BF16), jax.ShapeDtypeStruct((b, m, xa), BF16)),
        grid=(b,),
        in_specs=[pl.BlockSpec((None, m, d), lambda i: (i, 0, 0)), _resident((1, d)),
                  _resident(w_kt.shape), _resident(w_v.shape)],
        out_specs=(pl.BlockSpec((None, xa, m), lambda i: (i, 0, 0)),
                   pl.BlockSpec((None, m, xa), lambda i: (i, 0, 0))),
        compiler_params=pltpu.CompilerParams(
            dimension_semantics=("arbitrary",), vmem_limit_bytes=VMEM_LIMIT_BYTES),
        name="memkv",
    )(mem, g, w_kt, w_v)


def _log_sigmoid(x):
    return -(jnp.maximum(-x, 0.0) + jnp.log(1.0 + jnp.exp(-jnp.abs(x))))


def _split3(x):
    h1 = x.astype(BF16)
    r1 = x - h1.astype(F32)
    h2 = r1.astype(BF16)
    h3 = (r1 - h2.astype(F32)).astype(BF16)
    return h1, h2, h3


def _mix_kernel(x_ref, pre_g_ref, w_main_ref, w_kft_ref, w_fut_ref, bft_ref, gla_g_ref,
                w_pool_ref, pool_scale_ref, kt_ref, v_ref, w_up_gla_ref, w_up_pool_ref,
                w_up_x_ref, w_o_ref, post_g_ref, o_ref, state_ref, hist_ref, oscr_ref,
                *, tm, d, dk, dv, pw, xw):
    q0, v0 = 0, dk
    g0 = v0 + dv
    p0 = g0 + dv
    x0 = p0 + pw
    b0 = x0 + xw
    hdk, hdv = dk // GLA_HEADS, dv // GLA_HEADS
    s_idx = pl.program_id(1)

    @pl.when(s_idx == 0)
    def _():
        state_ref[...] = jnp.zeros_like(state_ref)
        hist_ref[...] = jnp.zeros_like(hist_ref)

    x = x_ref[...]
    h = _rms(x, pre_g_ref[...]).astype(BF16)

    kf = lax.dot_general(w_kft_ref[...], h, (((1,), (1,)), ((), ())),
                         preferred_element_type=F32)
    k_t = kf[0:dk]
    f_low_t = kf[dk:].astype(BF16)
    f_t = _dot(w_fut_ref[...], f_low_t) + bft_ref[...]
    la_t = _log_sigmoid(f_t) * (1.0 / GLA_GATE_TEMP)

    row = lax.broadcasted_iota(jnp.int32, (MXU_DIM, MXU_DIM), 0)
    col = lax.broadcasted_iota(jnp.int32, (MXU_DIM, MXU_DIM), 1)
    later = jnp.where((row > col) & ((row // CHUNK) == (col // CHUNK)), 1.0, 0.0).astype(BF16)
    parts = _split3(la_t)
    r_t = jnp.concatenate(
        [sum(_dot(p[:, i:i + MXU_DIM], later) for p in parts) for i in range(0, tm, MXU_DIM)],
        axis=1)
    kt_t = k_t * jnp.exp(r_t)
    dec_t = jnp.exp(r_t + la_t)
    lane = lax.broadcasted_iota(jnp.int32, (dk, tm), 1)
    first_half = (lane & CHUNK) == 0
    kt_even = jnp.where(first_half, kt_t, 0.0).astype(BF16)
    kt_odd = jnp.where(first_half, 0.0, kt_t).astype(BF16)

    q = (_dot(h, w_main_ref[:, q0:q0 + dk]) * (hdk ** -0.5)).astype(BF16)
    v = _dot(h, w_main_ref[:, v0:v0 + dv]).astype(BF16)
    for c in range(tm // CHUNK):
        j = (c * CHUNK) // LANES
        kt_src = kt_even if c % 2 == 0 else kt_odd
        for hd in range(GLA_HEADS):
            k_blk = kt_src[hd * hdk:(hd + 1) * hdk, j * LANES:(j + 1) * LANES]
            v_blk = v[j * LANES:(j + 1) * LANES, hd * hdv:(hd + 1) * hdv]
            dcol = dec_t[hd * hdk:(hd + 1) * hdk, c * CHUNK:c * CHUNK + 1]
            s_new = dcol * state_ref[hd] + _dot(k_blk, v_blk)
            state_ref[hd] = s_new
            q_c = q[c * CHUNK:(c + 1) * CHUNK, hd * hdk:(hd + 1) * hdk]
            oscr_ref[c * CHUNK:(c + 1) * CHUNK, hd * hdv:(hd + 1) * hdv] = _dot(
                q_c, s_new.astype(BF16))
    gla_g = gla_g_ref[...]
    o_n = jnp.concatenate(
        [_rms(oscr_ref[:, i:i + hdv], gla_g[:, i:i + hdv]) for i in range(0, dv, hdv)], axis=1)
    g_out = _dot(h, w_main_ref[:, g0:g0 + dv])
    y_a = _dot((o_n * (g_out * jax.nn.sigmoid(g_out))).astype(BF16), w_up_gla_ref[...])
    gate = jax.nn.sigmoid(_dot(h, w_main_ref[:, b0:b0 + d]))
    merged = gate * y_a

    p_in = _dot(h, w_main_ref[:, p0:p0 + pw])
    ext = jnp.concatenate([hist_ref[...], p_in], axis=0)
    hist_ref[...] = p_in[tm - POOL_HIST:, :]
    gd = pw // len(POOL_WINDOWS)
    pos = (lax.broadcasted_iota(jnp.int32, (tm, gd), 0) + (s_idx * tm + 1)).astype(F32)
    mixed = []
    for g, w in enumerate(POOL_WINDOWS):
        tok = ext[:, g * gd:(g + 1) * gd]
        win, span = tok, 1
        while span < w:
            win = win + pltpu.roll(win, span, 0)
            span *= 2
        pooled = win[POOL_HIST:, :] / jnp.minimum(pos, float(w)) - tok[POOL_HIST:, :]
        mixed.append(_dot(pooled.astype(BF16), w_pool_ref[g]))
    y_pool = (jnp.concatenate(mixed, axis=1) * pool_scale_ref[...]).astype(BF16)
    y_b = _dot(y_pool, w_up_pool_ref[...])
    gate = jax.nn.sigmoid(_dot(h, w_main_ref[:, b0 + d:b0 + 2 * d]))
    merged = merged + gate * y_b

    xq = _dot(h, w_main_ref[:, x0:x0 + xw]).astype(BF16)
    hx = xw // XA_HEADS
    heads = []
    for hd in range(XA_HEADS):
        s = _dot(xq[:, hd * hx:(hd + 1) * hx], kt_ref[hd * hx:(hd + 1) * hx, :]) * (hx ** -0.5)
        e = jnp.exp(s - jnp.max(s, axis=-1, keepdims=True))
        pr = (e / jnp.sum(e, axis=-1, keepdims=True)).astype(BF16)
        heads.append(_dot(pr, v_ref[:, hd * hx:(hd + 1) * hx]))
    y_c = _dot(jnp.concatenate(heads, axis=1).astype(BF16), w_up_x_ref[...])
    gate = jax.nn.sigmoid(_dot(h, w_main_ref[:, b0 + 2 * d:b0 + 3 * d]))
    merged = merged + gate * y_c

    y = _dot(merged.astype(BF16), w_o_ref[...])
    o_ref[...] = x + _rms(y, post_g_ref[...])


def _mix(x, pre_g, w_main, w_kft, w_fut, bft, gla_g, w_pool, pool_scale, kt, v, w_up_gla,
         w_up_pool, w_up_x, w_o, post_g, *, tm):
    b, s, d = x.shape
    dk = w_fut.shape[0]
    dv = w_up_gla.shape[0]
    pw = w_up_pool.shape[0]
    xw = w_up_x.shape[0]
    m = v.shape[1]
    assert s % tm == 0 and tm % MXU_DIM == 0
    assert w_main.shape[1] == dk + 2 * dv + pw + xw + N_BRANCH * d
    hdk, hdv = dk // GLA_HEADS, dv // GLA_HEADS
    tile = pl.BlockSpec((None, tm, d), lambda i, j: (i, j, 0))
    vec = _resident((1, d))
    in_specs = [
        tile, vec, _resident(w_main.shape), _resident(w_kft.shape), _resident(w_fut.shape),
        _resident(bft.shape), _resident(gla_g.shape), _resident(w_pool.shape),
        _resident(pool_scale.shape),
        pl.BlockSpec((None, xw, m), lambda i, j: (i, 0, 0)),
        pl.BlockSpec((None, m, xw), lambda i, j: (i, 0, 0)),
        _resident(w_up_gla.shape), _resident(w_up_pool.shape), _resident(w_up_x.shape),
        _resident(w_o.shape), vec,
    ]
    return pl.pallas_call(
        functools.partial(_mix_kernel, tm=tm, d=d, dk=dk, dv=dv, pw=pw, xw=xw),
        out_shape=jax.ShapeDtypeStruct((b, s, d), F32),
        grid=(b, s // tm),
        in_specs=in_specs,
        out_specs=tile,
        scratch_shapes=[
            pltpu.VMEM((GLA_HEADS, hdk, hdv), F32),
            pltpu.VMEM((POOL_HIST, pw), F32),
            pltpu.VMEM((tm, dv), F32),
        ],
        compiler_params=pltpu.CompilerParams(
            dimension_semantics=("arbitrary", "arbitrary"), vmem_limit_bytes=VMEM_LIMIT_BYTES),
        name="mix",
    )(x, pre_g, w_main, w_kft, w_fut, bft, gla_g, w_pool, pool_scale, kt, v, w_up_gla,
      w_up_pool, w_up_x, w_o, post_g)


def _pad_to(a, shape):
    return jnp.pad(a, [(0, t - s) for s, t in zip(a.shape, shape)])


def kernel(x, mem, ffn1_pre_g, ffn1_w_in, ffn1_w_out, ffn1_post_g, mix_pre_g, w_in, w_fu, b_f, gla_norm_g, w_pool, pool_scale, mem_norm_g, w_mem_kv, w_up_gla, w_up_pool, w_up_xattn, w_o, mix_post_g, ffn2_pre_g, ffn2_w_in, ffn2_w_out, ffn2_post_g, final_g):
    b, s, d = x.shape
    tm = 512
    rank, dk = w_fu.shape[1], w_fu.shape[2]
    dv = w_up_gla.shape[1]
    pw = w_up_pool.shape[1]
    xw = w_up_xattn.shape[1]
    for l in range(ffn1_w_in.shape[0]):
        vec = lambda a: a[l].reshape(1, -1)
        x2d = _ffn(x.reshape(b * s, d), vec(ffn1_pre_g), ffn1_w_in[l].astype(BF16),
                   ffn1_w_out[l].astype(BF16), vec(ffn1_post_g), None, tm=tm)

        w = w_in[l]
        edges = [0, dk, 2 * dk, 2 * dk + dv, 2 * dk + 2 * dv]
        edges += [edges[-1] + rank]
        edges += [edges[-1] + pw]
        edges += [edges[-1] + xw, w.shape[1]]
        w_q, w_k, w_v, w_g, w_f, w_p, w_x, w_b = (
            w[:, lo:hi] for lo, hi in zip(edges[:-1], edges[1:]))
        w_main = jnp.concatenate([w_q, w_v, w_g, w_p, w_x, w_b], axis=1).astype(BF16)
        w_kft = jnp.concatenate([w_k.T, _pad_to(w_f.T, (LANES, d))], axis=0).astype(BF16)
        w_fut = _pad_to(w_fu[l].T, (dk, LANES)).astype(BF16)
        w_kv = w_mem_kv[l]
        kt, v = _memkv(mem, vec(mem_norm_g), w_kv[:, :xw].T.astype(BF16),
                       w_kv[:, xw:].astype(BF16))
        x = _mix(x2d.reshape(b, s, d), vec(mix_pre_g), w_main, w_kft, w_fut,
                 b_f[l].reshape(dk, 1), vec(gla_norm_g), w_pool[l].astype(BF16),
                 vec(pool_scale), kt, v, w_up_gla[l].astype(BF16), w_up_pool[l].astype(BF16),
                 w_up_xattn[l].astype(BF16), w_o[l].astype(BF16), vec(mix_post_g), tm=tm)

        x = _ffn(x.reshape(b * s, d), vec(ffn2_pre_g), ffn2_w_in[l].astype(BF16),
                 ffn2_w_out[l].astype(BF16), vec(ffn2_post_g), vec(final_g), tm=tm
                 ).reshape(b, s, d)
    return x
```

```python
import functools

import jax
import jax.numpy as jnp
from jax import lax
from jax.experimental import pallas as pl
from jax.experimental.pallas import tpu as pltpu

EPS = 1e-6
CHUNK = 64
GLA_HEADS = 4
GLA_GATE_TEMP = 16.0
POOL_WINDOWS = (2, 4, 8, 16)
POOL_HIST = 16
XA_HEADS = 4
N_BRANCH = 3

LANES = 128
MXU_DIM = 256
VMEM_LIMIT_BYTES = 56 * 1024 * 1024

F32 = jnp.float32
BF16 = jnp.bfloat16


def _rms(x, g):
    return x * lax.rsqrt(jnp.mean(x * x, axis=-1, keepdims=True) + EPS) * g


def _dot(a, b):
    return jnp.dot(a, b, preferred_element_type=F32)


def _resident(shape):
    return pl.BlockSpec(shape, lambda *_: (0,) * len(shape), pipeline_mode=pl.Buffered(1))


def _ffn_chunks(d_ff):
    step = 4 * MXU_DIM
    return [(c0, min(step, d_ff - c0)) for c0 in range(0, d_ff, step)]


def _ffn_kernel(x_ref, pre_g_ref, w_in_ref, w_out_ref, post_g_ref, *rest, d_ff, final_norm):
    if final_norm:
        final_g_ref, o_ref = rest
    else:
        (o_ref,) = rest
    x = x_ref[...]
    h = _rms(x, pre_g_ref[...]).astype(BF16)
    acc = None
    for c0, cn in _ffn_chunks(d_ff):
        a = _dot(h, w_in_ref[:, c0:c0 + cn])
        b = _dot(h, w_in_ref[:, d_ff + c0:d_ff + c0 + cn])
        g = (a * jax.nn.sigmoid(a) * b).astype(BF16)
        p = _dot(g, w_out_ref[c0:c0 + cn, :])
        acc = p if acc is None else acc + p
    y = x + 0.5 * _rms(acc, post_g_ref[...])
    if final_norm:
        y = _rms(y, final_g_ref[...])
    o_ref[...] = y


def _ffn(x2d, pre_g, w_in, w_out, post_g, final_g, *, tm):
    t, d = x2d.shape
    d_ff = w_out.shape[0]
    assert t % tm == 0 and d_ff % MXU_DIM == 0
    final_norm = final_g is not None
    row = pl.BlockSpec((tm, d), lambda i: (i, 0))
    vec = _resident((1, d))
    in_specs = [row, vec, _resident(w_in.shape), _resident(w_out.shape), vec]
    args = [x2d, pre_g, w_in, w_out, post_g]
    if final_norm:
        in_specs.append(vec)
        args.append(final_g)
    return pl.pallas_call(
        functools.partial(_ffn_kernel, d_ff=d_ff, final_norm=final_norm),
        out_shape=jax.ShapeDtypeStruct((t, d), F32),
        grid=(t // tm,),
        in_specs=in_specs,
        out_specs=row,
        compiler_params=pltpu.CompilerParams(
            dimension_semantics=("arbitrary",), vmem_limit_bytes=VMEM_LIMIT_BYTES),
        name="ffn_final" if final_norm else "ffn",
    )(*args)


def _memkv_kernel(mem_ref, g_ref, w_kt_ref, w_v_ref, kt_ref, v_ref):
    mem_n = _rms(mem_ref[...], g_ref[...]).astype(BF16)
    kt = lax.dot_general(w_kt_ref[...], mem_n, (((1,), (1,)), ((), ())),
                         preferred_element_type=F32)
    kt_ref[...] = kt.astype(BF16)
    v_ref[...] = _dot(mem_n, w_v_ref[...]).astype(BF16)


def _memkv(mem, g, w_kt, w_v):
    b, m, d = mem.shape
    xa = w_v.shape[1]
    return pl.pallas_call(
        _memkv_kernel,
        out_shape=(jax.ShapeDtypeStruct((b, xa, m), BF16), jax.ShapeDtypeStruct((b, m, xa), BF16)),
        grid=(b,),
        in_specs=[pl.BlockSpec((None, m, d), lambda i: (i, 0, 0)), _resident((1, d)),
                  _resident(w_kt.shape), _resident(w_v.shape)],
        out_specs=(pl.BlockSpec((None, xa, m), lambda i: (i, 0, 0)),
                   pl.BlockSpec((None, m, xa), lambda i: (i, 0, 0))),
        compiler_params=pltpu.CompilerParams(
            dimension_semantics=("arbitrary",), vmem_limit_bytes=VMEM_LIMIT_BYTES),
        name="memkv",
    )(mem, g, w_kt, w_v)


def _log_sigmoid(x):
    return -(jnp.maximum(-x, 0.0) + jnp.log(1.0 + jnp.exp(-jnp.abs(x))))


def _split3(x):
    h1 = x.astype(BF16)
    r1 = x - h1.astype(F32)
    h2 = r1.astype(BF16)
    h3 = (r1 - h2.astype(F32)).astype(BF16)
    return h1, h2, h3


def _mix_kernel(x_ref, pre_g_ref, w_main_ref, w_kft_ref, w_fut_ref, bft_ref, gla_g_ref,
                w_pool_ref, pool_scale_ref, kt_ref, v_ref, w_up_gla_ref, w_up_pool_ref,
                w_up_x_ref, w_o_ref, post_g_ref, o_ref, state_ref, hist_ref, oscr_ref, kv_ref,
                *, tm, d, dk, dv, pw, xw):
    q0, v0 = 0, dk
    g0 = v0 + dv
    p0 = g0 + dv
    x0 = p0 + pw
    b0 = x0 + xw
    hdk, hdv = dk // GLA_HEADS, dv // GLA_HEADS
    s_idx = pl.program_id(1)

    @pl.when(s_idx == 0)
    def _():
        state_ref[...] = jnp.zeros_like(state_ref)
        hist_ref[...] = jnp.zeros_like(hist_ref)

    x = x_ref[...]
    h = _rms(x, pre_g_ref[...]).astype(BF16)

    kf = lax.dot_general(w_kft_ref[...], h, (((1,), (1,)), ((), ())),
                         preferred_element_type=F32)
    k_t = kf[0:dk]
    f_low_t = kf[dk:].astype(BF16)
    f_t = _dot(w_fut_ref[...], f_low_t) + bft_ref[...]
    la_t = _log_sigmoid(f_t) * (1.0 / GLA_GATE_TEMP)

    row = lax.broadcasted_iota(jnp.int32, (MXU_DIM, MXU_DIM), 0)
    col = lax.broadcasted_iota(jnp.int32, (MXU_DIM, MXU_DIM), 1)
    later = jnp.where((row > col) & ((row // CHUNK) == (col // CHUNK)), 1.0, 0.0).astype(BF16)
    parts = _split3(la_t)
    r_t = jnp.concatenate(
        [sum(_dot(p[:, i:i + MXU_DIM], later) for p in parts) for i in range(0, tm, MXU_DIM)],
        axis=1)
    kt_t = k_t * jnp.exp(r_t)
    dec_t = jnp.exp(r_t + la_t)
    lane = lax.broadcasted_iota(jnp.int32, (dk, tm), 1)
    first_half = (lane & CHUNK) == 0
    kt_even = jnp.where(first_half, kt_t, 0.0).astype(BF16)
    kt_odd = jnp.where(first_half, 0.0, kt_t).astype(BF16)

    q = (_dot(h, w_main_ref[:, q0:q0 + dk]) * (hdk ** -0.5)).astype(BF16)
    v = _dot(h, w_main_ref[:, v0:v0 + dv]).astype(BF16)
    n_chunks = tm // CHUNK
    for c in range(n_chunks):
        j = (c * CHUNK) // LANES
        kt_src = kt_even if c % 2 == 0 else kt_odd
        for hd in range(GLA_HEADS):
            k_blk = kt_src[hd * hdk:(hd + 1) * hdk, j * LANES:(j + 1) * LANES]
            v_blk = v[j * LANES:(j + 1) * LANES, hd * hdv:(hd + 1) * hdv]
            kv_ref[c * GLA_HEADS + hd] = _dot(k_blk, v_blk)
    for hd in range(GLA_HEADS):
        state = state_ref[hd]
        for c in range(n_chunks):
            dcol = dec_t[hd * hdk:(hd + 1) * hdk, c * CHUNK:c * CHUNK + 1]
            state = dcol * state + kv_ref[c * GLA_HEADS + hd]
            q_c = q[c * CHUNK:(c + 1) * CHUNK, hd * hdk:(hd + 1) * hdk]
            oscr_ref[c * CHUNK:(c + 1) * CHUNK, hd * hdv:(hd + 1) * hdv] = _dot(
                q_c, state.astype(BF16))
        state_ref[hd] = state
    gla_g = gla_g_ref[...]
    o_n = jnp.concatenate(
        [_rms(oscr_ref[:, i:i + hdv], gla_g[:, i:i + hdv]) for i in range(0, dv, hdv)], axis=1)
    g_out = _dot(h, w_main_ref[:, g0:g0 + dv])
    y_a = _dot((o_n * (g_out * jax.nn.sigmoid(g_out))).astype(BF16), w_up_gla_ref[...])
    gate = jax.nn.sigmoid(_dot(h, w_main_ref[:, b0:b0 + d]))
    merged = gate * y_a

    p_in = _dot(h, w_main_ref[:, p0:p0 + pw])
    ext = jnp.concatenate([hist_ref[...], p_in], axis=0)
    hist_ref[...] = p_in[tm - POOL_HIST:, :]
    gd = pw // len(POOL_WINDOWS)
    pos = (lax.broadcasted_iota(jnp.int32, (tm, gd), 0) + (s_idx * tm + 1)).astype(F32)
    mixed = []
    for g, w in enumerate(POOL_WINDOWS):
        tok = ext[:, g * gd:(g + 1) * gd]
        win, span = tok, 1
        while span < w:
            win = win + pltpu.roll(win, span, 0)
            span *= 2
        pooled = win[POOL_HIST:, :] / jnp.minimum(pos, float(w)) - tok[POOL_HIST:, :]
        mixed.append(_dot(pooled.astype(BF16), w_pool_ref[g]))
    y_pool = (jnp.concatenate(mixed, axis=1) * pool_scale_ref[...]).astype(BF16)
    y_b = _dot(y_pool, w_up_pool_ref[...])
    gate = jax.nn.sigmoid(_dot(h, w_main_ref[:, b0 + d:b0 + 2 * d]))
    merged = merged + gate * y_b

    xq = _dot(h, w_main_ref[:, x0:x0 + xw]).astype(BF16)
    hx = xw // XA_HEADS
    heads = []
    for hd in range(XA_HEADS):
        s = _dot(xq[:, hd * hx:(hd + 1) * hx], kt_ref[hd * hx:(hd + 1) * hx, :]) * (hx ** -0.5)
        e = jnp.exp(s - jnp.max(s, axis=-1, keepdims=True))
        pr = (e / jnp.sum(e, axis=-1, keepdims=True)).astype(BF16)
        heads.append(_dot(pr, v_ref[:, hd * hx:(hd + 1) * hx]))
    y_c = _dot(jnp.concatenate(heads, axis=1).astype(BF16), w_up_x_ref[...])
    gate = jax.nn.sigmoid(_dot(h, w_main_ref[:, b0 + 2 * d:b0 + 3 * d]))
    merged = merged + gate * y_c

    y = _dot(merged.astype(BF16), w_o_ref[...])
    o_ref[...] = x + _rms(y, post_g_ref[...])


def _mix(x, pre_g, w_main, w_kft, w_fut, bft, gla_g, w_pool, pool_scale, kt, v, w_up_gla,
         w_up_pool, w_up_x, w_o, post_g, *, tm):
    b, s, d = x.shape
    dk = w_fut.shape[0]
    dv = w_up_gla.shape[0]
    pw = w_up_pool.shape[0]
    xw = w_up_x.shape[0]
    m = v.shape[1]
    assert s % tm == 0 and tm % MXU_DIM == 0
    assert w_main.shape[1] == dk + 2 * dv + pw + xw + N_BRANCH * d
    hdk, hdv = dk // GLA_HEADS, dv // GLA_HEADS
    tile = pl.BlockSpec((None, tm, d), lambda i, j: (i, j, 0))
    vec = _resident((1, d))
    in_specs = [
        tile, vec, _resident(w_main.shape), _resident(w_kft.shape), _resident(w_fut.shape),
        _resident(bft.shape), _resident(gla_g.shape), _resident(w_pool.shape),
        _resident(pool_scale.shape),
        pl.BlockSpec((None, xw, m), lambda i, j: (i, 0, 0)),
        pl.BlockSpec((None, m, xw), lambda i, j: (i, 0, 0)),
        _resident(w_up_gla.shape), _resident(w_up_pool.shape), _resident(w_up_x.shape),
        _resident(w_o.shape), vec,
    ]
    return pl.pallas_call(
        functools.partial(_mix_kernel, tm=tm, d=d, dk=dk, dv=dv, pw=pw, xw=xw),
        out_shape=jax.ShapeDtypeStruct((b, s, d), F32),
        grid=(b, s // tm),
        in_specs=in_specs,
        out_specs=tile,
        scratch_shapes=[
            pltpu.VMEM((GLA_HEADS, hdk, hdv), F32),
            pltpu.VMEM((POOL_HIST, pw), F32),
            pltpu.VMEM((tm, dv), F32),
            pltpu.VMEM((tm // CHUNK * GLA_HEADS, hdk, hdv), F32),
        ],
        compiler_params=pltpu.CompilerParams(
            dimension_semantics=("arbitrary", "arbitrary"), vmem_limit_bytes=VMEM_LIMIT_BYTES),
        name="mix",
    )(x, pre_g, w_main, w_kft, w_fut, bft, gla_g, w_pool, pool_scale, kt, v, w_up_gla,
      w_up_pool, w_up_x, w_o, post_g)


def _pad_to(a, shape):
    return jnp.pad(a, [(0, t - s) for s, t in zip(a.shape, shape)])


def kernel(x, mem, ffn1_pre_g, ffn1_w_in, ffn1_w_out, ffn1_post_g, mix_pre_g, w_in, w_fu, b_f, gla_norm_g, w_pool, pool_scale, mem_norm_g, w_mem_kv, w_up_gla, w_up_pool, w_up_xattn, w_o, mix_post_g, ffn2_pre_g, ffn2_w_in, ffn2_w_out, ffn2_post_g, final_g):
    b, s, d = x.shape
    tm = 512
    rank, dk = w_fu.shape[1], w_fu.shape[2]
    dv = w_up_gla.shape[1]
    pw = w_up_pool.shape[1]
    xw = w_up_xattn.shape[1]
    for l in range(ffn1_w_in.shape[0]):
        vec = lambda a: a[l].reshape(1, -1)
        x2d = _ffn(x.reshape(b * s, d), vec(ffn1_pre_g), ffn1_w_in[l].astype(BF16),
                   ffn1_w_out[l].astype(BF16), vec(ffn1_post_g), None, tm=tm)

        w = w_in[l]
        edges = [0, dk, 2 * dk, 2 * dk + dv, 2 * dk + 2 * dv]
        edges += [edges[-1] + rank]
        edges += [edges[-1] + pw]
        edges += [edges[-1] + xw, w.shape[1]]
        w_q, w_k, w_v, w_g, w_f, w_p, w_x, w_b = (
            w[:, lo:hi] for lo, hi in zip(edges[:-1], edges[1:]))
        w_main = jnp.concatenate([w_q, w_v, w_g, w_p, w_x, w_b], axis=1).astype(BF16)
        w_kft = jnp.concatenate([w_k.T, _pad_to(w_f.T, (LANES, d))], axis=0).astype(BF16)
        w_fut = _pad_to(w_fu[l].T, (dk, LANES)).astype(BF16)
        w_kv = w_mem_kv[l]
        kt, v = _memkv(mem, vec(mem_norm_g), w_kv[:, :xw].T.astype(BF16),
                       w_kv[:, xw:].astype(BF16))
        x = _mix(x2d.reshape(b, s, d), vec(mix_pre_g), w_main, w_kft, w_fut,
                 b_f[l].reshape(dk, 1), vec(gla_norm_g), w_pool[l].astype(BF16),
                 vec(pool_scale), kt, v, w_up_gla[l].astype(BF16), w_up_pool[l].astype(BF16),
                 w_up_xattn[l].astype(BF16), w_o[l].astype(BF16), vec(mix_post_g), tm=tm)

        x = _ffn(x.reshape(b * s, d), vec(ffn2_pre_g), ffn2_w_in[l].astype(BF16),
                 ffn2_w_out[l].astype(BF16), vec(ffn2_post_g), vec(final_g), tm=tm
                 ).reshape(b, s, d)
    return x
```

```python
import functools

import jax
import jax.numpy as jnp
from jax import lax
from jax.experimental import pallas as pl
from jax.experimental.pallas import tpu as pltpu

EPS = 1e-6
CHUNK = 64
GLA_HEADS = 4
GLA_GATE_TEMP = 16.0
POOL_WINDOWS = (2, 4, 8, 16)
POOL_HIST = 16
XA_HEADS = 4
N_BRANCH = 3

LANES = 128
MXU_DIM = 256
VMEM_LIMIT_BYTES = 56 * 1024 * 1024
MIX_TM = 512
FFN_TM = 1024
FFN_SUB = 256

F32 = jnp.float32
BF16 = jnp.bfloat16


def _rms(x, g):
    return x * lax.rsqrt(jnp.mean(x * x, axis=-1, keepdims=True) + EPS) * g


def _dot(a, b):
    return jnp.dot(a, b, preferred_element_type=F32)


def _resident(shape):
    return pl.BlockSpec(shape, lambda *_: (0,) * len(shape), pipeline_mode=pl.Buffered(1))


def _ffn_chunks(d_ff):
    step = 4 * MXU_DIM
    return [(c0, min(step, d_ff - c0)) for c0 in range(0, d_ff, step)]


def _ffn_kernel(x_ref, pre_g_ref, w_in_ref, w_out_ref, post_g_ref, *rest, d_ff, final_norm,
                sub):
    if final_norm:
        final_g_ref, o_ref = rest
    else:
        (o_ref,) = rest
    for r in range(0, x_ref.shape[0], sub):
        x = x_ref[r:r + sub, :]
        h = _rms(x, pre_g_ref[...]).astype(BF16)
        acc = None
        for c0, cn in _ffn_chunks(d_ff):
            a = _dot(h, w_in_ref[:, c0:c0 + cn])
            b = _dot(h, w_in_ref[:, d_ff + c0:d_ff + c0 + cn])
            g = (a * jax.nn.sigmoid(a) * b).astype(BF16)
            p = _dot(g, w_out_ref[c0:c0 + cn, :])
            acc = p if acc is None else acc + p
        y = x + 0.5 * _rms(acc, post_g_ref[...])
        if final_norm:
            y = _rms(y, final_g_ref[...])
        o_ref[r:r + sub, :] = y


def _ffn(x2d, pre_g, w_in, w_out, post_g, final_g, *, tm, sub):
    t, d = x2d.shape
    d_ff = w_out.shape[0]
    assert t % tm == 0 and d_ff % MXU_DIM == 0
    final_norm = final_g is not None
    row = pl.BlockSpec((tm, d), lambda i: (i, 0))
    vec = _resident((1, d))
    in_specs = [row, vec, _resident(w_in.shape), _resident(w_out.shape), vec]
    args = [x2d, pre_g, w_in, w_out, post_g]
    if final_norm:
        in_specs.append(vec)
        args.append(final_g)
    return pl.pallas_call(
        functools.partial(_ffn_kernel, d_ff=d_ff, final_norm=final_norm, sub=sub),
        out_shape=jax.ShapeDtypeStruct((t, d), F32),
        grid=(t // tm,),
        in_specs=in_specs,
        out_specs=row,
        compiler_params=pltpu.CompilerParams(
            dimension_semantics=("arbitrary",), vmem_limit_bytes=VMEM_LIMIT_BYTES),
        name="ffn_final" if final_norm else "ffn",
    )(*args)


def _memkv_kernel(mem_ref, g_ref, w_kt_ref, w_v_ref, kt_ref, v_ref):
    mem_n = _rms(mem_ref[...], g_ref[...]).astype(BF16)
    kt = lax.dot_general(w_kt_ref[...], mem_n, (((1,), (1,)), ((), ())),
                         preferred_element_type=F32)
    kt_ref[...] = kt.astype(BF16)
    v_ref[...] = _dot(mem_n, w_v_ref[...]).astype(BF16)


def _memkv(mem, g, w_kt, w_v):
    b, m, d = mem.shape
    xa = w_v.shape[1]
    return pl.pallas_call(
        _memkv_kernel,
        out_shape=(jax.ShapeDtypeStruct((b, xa, m), BF16), jax.ShapeDtypeStruct((b, m, xa), BF16)),
        grid=(b,),
        in_specs=[pl.BlockSpec((None, m, d), lambda i: (i, 0, 0)), _resident((1, d)),
                  _resident(w_kt.shape), _resident(w_v.shape)],
        out_specs=(pl.BlockSpec((None, xa, m), lambda i: (i, 0, 0)),
                   pl.BlockSpec((None, m, xa), lambda i: (i, 0, 0))),
        compiler_params=pltpu.CompilerParams(
            dimension_semantics=("arbitrary",), vmem_limit_bytes=VMEM_LIMIT_BYTES),
        name="memkv",
    )(mem, g, w_kt, w_v)


def _log_sigmoid(x):
    return -(jnp.maximum(-x, 0.0) + jnp.log(1.0 + jnp.exp(-jnp.abs(x))))


def _split3(x):
    h1 = x.astype(BF16)
    r1 = x - h1.astype(F32)
    h2 = r1.astype(BF16)
    h3 = (r1 - h2.astype(F32)).astype(BF16)
    return h1, h2, h3


def _mix_kernel(x_ref, pre_g_ref, w_main_ref, w_kft_ref, w_fut_ref, bft_ref, gla_g_ref,
                w_pool_ref, pool_scale_ref, kt_ref, v_ref, w_up_gla_ref, w_up_pool_ref,
                w_up_x_ref, w_o_ref, post_g_ref, o_ref, state_ref, hist_ref, oscr_ref, kv_ref,
                *, tm, d, dk, dv, pw, xw):
    q0, v0 = 0, dk
    g0 = v0 + dv
    p0 = g0 + dv
    x0 = p0 + pw
    b0 = x0 + xw
    hdk, hdv = dk // GLA_HEADS, dv // GLA_HEADS
    s_idx = pl.program_id(1)

    @pl.when(s_idx == 0)
    def _():
        state_ref[...] = jnp.zeros_like(state_ref)
        hist_ref[...] = jnp.zeros_like(hist_ref)

    x = x_ref[...]
    h = _rms(x, pre_g_ref[...]).astype(BF16)

    def proj(c0, width):
        return _dot(h, w_main_ref[:, c0:c0 + width])

    q = (proj(q0, dk) * (hdk ** -0.5)).astype(BF16)
    v = proj(v0, dv).astype(BF16)

    kf = lax.dot_general(w_kft_ref[...], h, (((1,), (1,)), ((), ())),
                         preferred_element_type=F32)
    f_low_t = kf[0:LANES].astype(BF16)
    k_t = kf[LANES:]
    f_t = _dot(w_fut_ref[...], f_low_t) + bft_ref[...]
    g_out = proj(g0, dv)
    p_in = proj(p0, pw)
    xq = proj(x0, xw).astype(BF16)
    la_t = _log_sigmoid(f_t) * (1.0 / GLA_GATE_TEMP)

    row = lax.broadcasted_iota(jnp.int32, (MXU_DIM, MXU_DIM), 0)
    col = lax.broadcasted_iota(jnp.int32, (MXU_DIM, MXU_DIM), 1)
    later = jnp.where((row > col) & ((row // CHUNK) == (col // CHUNK)), 1.0, 0.0).astype(BF16)
    parts = _split3(la_t)
    r_t = jnp.concatenate(
        [sum(_dot(p[:, i:i + MXU_DIM], later) for p in parts) for i in range(0, tm, MXU_DIM)],
        axis=1)
    gate_a = jax.nn.sigmoid(proj(b0, d))
    gated = g_out * jax.nn.sigmoid(g_out)
    kt_t = k_t * jnp.exp(r_t)
    dec_t = jnp.exp(r_t + la_t)
    lane = lax.broadcasted_iota(jnp.int32, (dk, tm), 1)
    first_half = (lane & CHUNK) == 0
    kt_even = jnp.where(first_half, kt_t, 0.0).astype(BF16)
    kt_odd = jnp.where(first_half, 0.0, kt_t).astype(BF16)

    n_chunks = tm // CHUNK
    for c in range(n_chunks):
        j = (c * CHUNK) // LANES
        kt_src = kt_even if c % 2 == 0 else kt_odd
        for hd in range(GLA_HEADS):
            k_blk = kt_src[hd * hdk:(hd + 1) * hdk, j * LANES:(j + 1) * LANES]
            v_blk = v[j * LANES:(j + 1) * LANES, hd * hdv:(hd + 1) * hdv]
            kv_ref[c * GLA_HEADS + hd] = _dot(k_blk, v_blk)
    ext = jnp.concatenate([hist_ref[...], p_in], axis=0)
    hist_ref[...] = p_in[tm - POOL_HIST:, :]
    gd = pw // len(POOL_WINDOWS)
    pos = (lax.broadcasted_iota(jnp.int32, (tm, gd), 0) + (s_idx * tm + 1)).astype(F32)
    gate_b = jax.nn.sigmoid(proj(b0 + d, d))
    mixed = []
    for g, w in enumerate(POOL_WINDOWS):
        tok = ext[:, g * gd:(g + 1) * gd]
        win, span = tok, 1
        while span < w:
            win = win + pltpu.roll(win, span, 0)
            span *= 2
        pooled = win[POOL_HIST:, :] / jnp.minimum(pos, float(w)) - tok[POOL_HIST:, :]
        mixed.append(_dot(pooled.astype(BF16), w_pool_ref[g]))
    y_pool = (jnp.concatenate(mixed, axis=1) * pool_scale_ref[...]).astype(BF16)

    for hd in range(GLA_HEADS):
        state = state_ref[hd]
        for c in range(n_chunks):
            dcol = dec_t[hd * hdk:(hd + 1) * hdk, c * CHUNK:c * CHUNK + 1]
            state = dcol * state + kv_ref[c * GLA_HEADS + hd]
            q_c = q[c * CHUNK:(c + 1) * CHUNK, hd * hdk:(hd + 1) * hdk]
            oscr_ref[c * CHUNK:(c + 1) * CHUNK, hd * hdv:(hd + 1) * hdv] = _dot(
                q_c, state.astype(BF16))
        state_ref[hd] = state

    hx = xw // XA_HEADS
    scores = [_dot(xq[:, hd * hx:(hd + 1) * hx], kt_ref[hd * hx:(hd + 1) * hx, :])
              for hd in range(XA_HEADS)]
    gate_c = jax.nn.sigmoid(proj(b0 + 2 * d, d))
    y_b = _dot(y_pool, w_up_pool_ref[...])
    gla_g = gla_g_ref[...]
    o_n = jnp.concatenate(
        [_rms(oscr_ref[:, i:i + hdv], gla_g[:, i:i + hdv]) for i in range(0, dv, hdv)], axis=1)
    y_a = _dot((o_n * gated).astype(BF16), w_up_gla_ref[...])
    heads = []
    for hd in range(XA_HEADS):
        s = scores[hd] * (hx ** -0.5)
        e = jnp.exp(s - jnp.max(s, axis=-1, keepdims=True))
        pr = (e / jnp.sum(e, axis=-1, keepdims=True)).astype(BF16)
        heads.append(_dot(pr, v_ref[:, hd * hx:(hd + 1) * hx]))
    y_c = _dot(jnp.concatenate(heads, axis=1).astype(BF16), w_up_x_ref[...])
    merged = (gate_a * y_a + gate_b * y_b + gate_c * y_c).astype(BF16)

    half = tm // 2
    for r in range(0, tm, half):
        y = _dot(merged[r:r + half], w_o_ref[...])
        o_ref[r:r + half, :] = x[r:r + half] + _rms(y, post_g_ref[...])


def _mix(x, pre_g, w_main, w_kft, w_fut, bft, gla_g, w_pool, pool_scale, kt, v, w_up_gla,
         w_up_pool, w_up_x, w_o, post_g, *, tm):
    b, s, d = x.shape
    dk = w_fut.shape[0]
    dv = w_up_gla.shape[0]
    pw = w_up_pool.shape[0]
    xw = w_up_x.shape[0]
    m = v.shape[1]
    assert s % tm == 0 and tm % MXU_DIM == 0
    assert w_main.shape[1] == dk + 2 * dv + pw + xw + N_BRANCH * d
    hdk, hdv = dk // GLA_HEADS, dv // GLA_HEADS
    tile = pl.BlockSpec((None, tm, d), lambda i, j: (i, j, 0))
    vec = _resident((1, d))
    in_specs = [
        tile, vec, _resident(w_main.shape), _resident(w_kft.shape), _resident(w_fut.shape),
        _resident(bft.shape), _resident(gla_g.shape), _resident(w_pool.shape),
        _resident(pool_scale.shape),
        pl.BlockSpec((None, xw, m), lambda i, j: (i, 0, 0)),
        pl.BlockSpec((None, m, xw), lambda i, j: (i, 0, 0)),
        _resident(w_up_gla.shape), _resident(w_up_pool.shape), _resident(w_up_x.shape),
        _resident(w_o.shape), vec,
    ]
    return pl.pallas_call(
        functools.partial(_mix_kernel, tm=tm, d=d, dk=dk, dv=dv, pw=pw, xw=xw),
        out_shape=jax.ShapeDtypeStruct((b, s, d), F32),
        grid=(b, s // tm),
        in_specs=in_specs,
        out_specs=tile,
        scratch_shapes=[
            pltpu.VMEM((GLA_HEADS, hdk, hdv), F32),
            pltpu.VMEM((POOL_HIST, pw), F32),
            pltpu.VMEM((tm, dv), F32),
            pltpu.VMEM((tm // CHUNK * GLA_HEADS, hdk, hdv), F32),
        ],
        compiler_params=pltpu.CompilerParams(
            dimension_semantics=("arbitrary", "arbitrary"), vmem_limit_bytes=VMEM_LIMIT_BYTES),
        name="mix",
    )(x, pre_g, w_main, w_kft, w_fut, bft, gla_g, w_pool, pool_scale, kt, v, w_up_gla,
      w_up_pool, w_up_x, w_o, post_g)


def _pad_to(a, shape):
    return jnp.pad(a, [(0, t - s) for s, t in zip(a.shape, shape)])


def kernel(x, mem, ffn1_pre_g, ffn1_w_in, ffn1_w_out, ffn1_post_g, mix_pre_g, w_in, w_fu, b_f, gla_norm_g, w_pool, pool_scale, mem_norm_g, w_mem_kv, w_up_gla, w_up_pool, w_up_xattn, w_o, mix_post_g, ffn2_pre_g, ffn2_w_in, ffn2_w_out, ffn2_post_g, final_g):
    b, s, d = x.shape
    tm = MIX_TM
    rank, dk = w_fu.shape[1], w_fu.shape[2]
    dv = w_up_gla.shape[1]
    pw = w_up_pool.shape[1]
    xw = w_up_xattn.shape[1]
    for l in range(ffn1_w_in.shape[0]):
        vec = lambda a: a[l].reshape(1, -1)
        x2d = _ffn(x.reshape(b * s, d), vec(ffn1_pre_g), ffn1_w_in[l].astype(BF16),
                   ffn1_w_out[l].astype(BF16), vec(ffn1_post_g), None, tm=FFN_TM, sub=FFN_SUB)

        w = w_in[l]
        edges = [0, dk, 2 * dk, 2 * dk + dv, 2 * dk + 2 * dv]
        edges += [edges[-1] + rank]
        edges += [edges[-1] + pw]
        edges += [edges[-1] + xw, w.shape[1]]
        w_q, w_k, w_v, w_g, w_f, w_p, w_x, w_b = (
            w[:, lo:hi] for lo, hi in zip(edges[:-1], edges[1:]))
        w_main = jnp.concatenate([w_q, w_v, w_g, w_p, w_x, w_b], axis=1).astype(BF16)
        w_kft = jnp.concatenate([_pad_to(w_f.T, (LANES, d)), w_k.T], axis=0).astype(BF16)
        w_fut = _pad_to(w_fu[l].T, (dk, LANES)).astype(BF16)
        w_kv = w_mem_kv[l]
        kt, v = _memkv(mem, vec(mem_norm_g), w_kv[:, :xw].T.astype(BF16),
                       w_kv[:, xw:].astype(BF16))
        x = _mix(x2d.reshape(b, s, d), vec(mix_pre_g), w_main, w_kft, w_fut,
                 b_f[l].reshape(dk, 1), vec(gla_norm_g), w_pool[l].astype(BF16),
                 vec(pool_scale), kt, v, w_up_gla[l].astype(BF16), w_up_pool[l].astype(BF16),
                 w_up_xattn[l].astype(BF16), w_o[l].astype(BF16), vec(mix_post_g), tm=tm)

        x = _ffn(x.reshape(b * s, d), vec(ffn2_pre_g), ffn2_w_in[l].astype(BF16),
                 ffn2_w_out[l].astype(BF16), vec(ffn2_post_g), vec(final_g), tm=FFN_TM, sub=FFN_SUB
                 ).reshape(b, s, d)
    return x
```

```python
import functools

import jax
import jax.numpy as jnp
from jax import lax
from jax.experimental import pallas as pl
from jax.experimental.pallas import tpu as pltpu

EPS = 1e-6
CHUNK = 64
GLA_HEADS = 4
GLA_GATE_TEMP = 16.0
POOL_WINDOWS = (2, 4, 8, 16)
POOL_HIST = 16
XA_HEADS = 4
N_BRANCH = 3

LANES = 128
MXU_DIM = 256
VMEM_LIMIT_BYTES = 56 * 1024 * 1024
MIX_TM = 512
FFN_TM = 512
X_SLOTS = 4

F32 = jnp.float32
BF16 = jnp.bfloat16


def _rms(x, g):
    return x * lax.rsqrt(jnp.mean(x * x, axis=-1, keepdims=True) + EPS) * g


def _dot(a, b):
    return jnp.dot(a, b, preferred_element_type=F32)


def _zero_index_after(v):
    u = lax.bitcast_convert_type(v, jnp.uint32)
    z = lax.shift_right_logical(lax.shift_right_logical(u, jnp.uint32(16)), jnp.uint32(16))
    return z[0, 0].astype(jnp.int32)


def _resident(shape):
    return pl.BlockSpec(shape, lambda *_: (0,) * len(shape), pipeline_mode=pl.Buffered(1))


def _ffn_chunks(d_ff):
    step = 4 * MXU_DIM
    return [(c0, min(step, d_ff - c0)) for c0 in range(0, d_ff, step)]


def _ffn_kernel(x_hbm, pre_g_ref, w_in_ref, w_out_ref, post_g_ref, *rest, n_tiles, tm, d_ff,
                final_norm):
    if final_norm:
        final_g_ref, *rest = rest
    o_hbm, xbuf, h_cur, h_next, h_new, acc_ref, done_ref, obuf, in_sem, out_sem = rest
    chunks = _ffn_chunks(d_ff)

    def x_copy(tile, slot):
        return pltpu.make_async_copy(x_hbm.at[pl.ds(tile * tm, tm), :], xbuf.at[slot],
                                     in_sem.at[slot])

    def o_copy(tile, slot):
        return pltpu.make_async_copy(obuf.at[slot], o_hbm.at[pl.ds(tile * tm, tm), :],
                                     out_sem.at[slot])

    def norm_in(slot, h_ref, zeros=(0,)):
        rows = tm // len(zeros)
        for k, zero in enumerate(zeros):
            r = pl.ds(k * rows, rows)
            h_ref[r, :] = _rms(xbuf[slot + zero, r, :], pre_g_ref[...]).astype(BF16)

    def swiglu(h_ref, c0, cn):
        a = _dot(h_ref[...], w_in_ref[:, c0:c0 + cn])
        b = _dot(h_ref[...], w_in_ref[:, d_ff + c0:d_ff + c0 + cn])
        g = (a * jax.nn.sigmoid(a) * b).astype(BF16)
        marks = [m[tm - 8:, j:j + LANES] for m in (a, b) for j in range(0, cn, MXU_DIM)]
        return _dot(g, w_out_ref[c0:c0 + cn, :]), marks

    def matmuls_head():
        acc_ref[...], marks = swiglu(h_next, *chunks[0])
        return [_zero_index_after(m) for m in marks]

    def matmuls_tail():
        acc = acc_ref[...]
        for c0, cn in chunks[1:]:
            acc = acc + swiglu(h_cur, c0, cn)[0]
        done_ref[0] = acc

    def norm_out(x_slot, o_slot, zeros):
        rows = tm // len(zeros)
        for k, zero in enumerate(zeros):
            r = pl.ds(k * rows, rows)
            y = xbuf[x_slot + zero, r, :] + 0.5 * _rms(done_ref[zero, r, :], post_g_ref[...])
            if final_norm:
                y = _rms(y, final_g_ref[...])
            obuf[o_slot, r, :] = y

    for t in range(3):
        x_copy(min(t, n_tiles - 1), t).start()
    x_copy(0, 0).wait()
    norm_in(0, h_next)
    matmuls_head()
    h_cur[...] = h_next[...]
    x_copy(0, 1).wait()
    norm_in(1, h_next)

    def body(i, carry):
        o_slot = lax.rem(i, 2)

        @pl.when(i >= 2)
        def _():
            o_copy(0, o_slot).wait()
        x_copy(jnp.minimum(i + 3, n_tiles - 1), lax.rem(i + 3, X_SLOTS)).start()
        x_copy(0, lax.rem(i + 2, X_SLOTS)).wait()
        matmuls_tail()
        zeros = matmuls_head()
        norm_out(lax.rem(i, X_SLOTS), o_slot, zeros)
        norm_in(lax.rem(i + 2, X_SLOTS), h_new, zeros)
        h_cur[...] = h_next[...]
        h_next[...] = h_new[...]
        o_copy(i, o_slot).start()
        return carry

    lax.fori_loop(0, n_tiles, body, 0)
    o_copy(0, (n_tiles - 2) % 2).wait()
    o_copy(0, (n_tiles - 1) % 2).wait()
    x_copy(0, (n_tiles + 2) % X_SLOTS).wait()


def _ffn(x2d, pre_g, w_in, w_out, post_g, final_g, *, tm):
    t, d = x2d.shape
    d_ff = w_out.shape[0]
    assert t % tm == 0 and t // tm >= 2 and d_ff % MXU_DIM == 0
    final_norm = final_g is not None
    vmem = pl.BlockSpec(memory_space=pltpu.VMEM)
    hbm = pl.BlockSpec(memory_space=pl.ANY)
    args = [x2d, pre_g, w_in, w_out, post_g] + ([final_g] if final_norm else [])
    return pl.pallas_call(
        functools.partial(_ffn_kernel, n_tiles=t // tm, tm=tm, d_ff=d_ff,
                          final_norm=final_norm),
        out_shape=jax.ShapeDtypeStruct((t, d), F32),
        in_specs=[hbm] + [vmem] * (len(args) - 1),
        out_specs=hbm,
        scratch_shapes=[
            pltpu.VMEM((X_SLOTS, tm, d), F32),
            pltpu.VMEM((tm, d), BF16),
            pltpu.VMEM((tm, d), BF16),
            pltpu.VMEM((tm, d), BF16),
            pltpu.VMEM((tm, d), F32),
            pltpu.VMEM((1, tm, d), F32),
            pltpu.VMEM((2, tm, d), F32),
            pltpu.SemaphoreType.DMA((X_SLOTS,)),
            pltpu.SemaphoreType.DMA((2,)),
        ],
        compiler_params=pltpu.CompilerParams(vmem_limit_bytes=VMEM_LIMIT_BYTES),
        name="ffn_final" if final_norm else "ffn",
    )(*args)


def _memkv_kernel(mem_ref, g_ref, w_kt_ref, w_v_ref, kt_ref, v_ref):
    mem_n = _rms(mem_ref[...], g_ref[...]).astype(BF16)
    kt = lax.dot_general(w_kt_ref[...], mem_n, (((1,), (1,)), ((), ())),
                         preferred_element_type=F32)
    kt_ref[...] = kt.astype(BF16)
    v_ref[...] = _dot(mem_n, w_v_ref[...]).astype(BF16)


def _memkv(mem, g, w_kt, w_v):
    b, m, d = mem.shape
    xa = w_v.shape[1]
    return pl.pallas_call(
        _memkv_kernel,
        out_shape=(jax.ShapeDtypeStruct((b, xa, m), BF16), jax.ShapeDtypeStruct((b, m, xa), BF16)),
        grid=(b,),
        in_specs=[pl.BlockSpec((None, m, d), lambda i: (i, 0, 0)), _resident((1, d)),
                  _resident(w_kt.shape), _resident(w_v.shape)],
        out_specs=(pl.BlockSpec((None, xa, m), lambda i: (i, 0, 0)),
                   pl.BlockSpec((None, m, xa), lambda i: (i, 0, 0))),
        compiler_params=pltpu.CompilerParams(
            dimension_semantics=("arbitrary",), vmem_limit_bytes=VMEM_LIMIT_BYTES),
        name="memkv",
    )(mem, g, w_kt, w_v)


def _log_sigmoid(x):
    return -(jnp.maximum(-x, 0.0) + jnp.log(1.0 + jnp.exp(-jnp.abs(x))))


def _split3(x):
    h1 = x.astype(BF16)
    r1 = x - h1.astype(F32)
    h2 = r1.astype(BF16)
    h3 = (r1 - h2.astype(F32)).astype(BF16)
    return h1, h2, h3


def _mix_kernel(x_ref, pre_g_ref, w_main_ref, w_kft_ref, w_fut_ref, bft_ref, gla_g_ref,
                w_pool_ref, pool_scale_ref, kt_ref, v_ref, w_up_gla_ref, w_up_pool_ref,
                w_up_x_ref, w_o_ref, post_g_ref, o_ref, state_ref, hist_ref, oscr_ref, kv_ref,
                *, tm, d, dk, dv, pw, xw):
    q0, v0 = 0, dk
    g0 = v0 + dv
    p0 = g0 + dv
    x0 = p0 + pw
    b0 = x0 + xw
    hdk, hdv = dk // GLA_HEADS, dv // GLA_HEADS
    s_idx = pl.program_id(1)

    @pl.when(s_idx == 0)
    def _():
        state_ref[...] = jnp.zeros_like(state_ref)
        hist_ref[...] = jnp.zeros_like(hist_ref)

    x = x_ref[...]
    h = _rms(x, pre_g_ref[...]).astype(BF16)

    def proj(c0, width):
        return _dot(h, w_main_ref[:, c0:c0 + width])

    q = (proj(q0, dk) * (hdk ** -0.5)).astype(BF16)
    v = proj(v0, dv).astype(BF16)

    kf = lax.dot_general(w_kft_ref[...], h, (((1,), (1,)), ((), ())),
                         preferred_element_type=F32)
    f_low_t = kf[0:LANES].astype(BF16)
    k_t = kf[LANES:]
    f_t = _dot(w_fut_ref[...], f_low_t) + bft_ref[...]
    g_out = proj(g0, dv)
    p_in = proj(p0, pw)
    xq = proj(x0, xw).astype(BF16)
    la_t = _log_sigmoid(f_t) * (1.0 / GLA_GATE_TEMP)

    row = lax.broadcasted_iota(jnp.int32, (MXU_DIM, MXU_DIM), 0)
    col = lax.broadcasted_iota(jnp.int32, (MXU_DIM, MXU_DIM), 1)
    later = jnp.where((row > col) & ((row // CHUNK) == (col // CHUNK)), 1.0, 0.0).astype(BF16)
    parts = _split3(la_t)
    r_t = jnp.concatenate(
        [sum(_dot(p[:, i:i + MXU_DIM], later) for p in parts) for i in range(0, tm, MXU_DIM)],
        axis=1)
    gate_a = jax.nn.sigmoid(proj(b0, d))
    gated = g_out * jax.nn.sigmoid(g_out)
    kt_t = k_t * jnp.exp(r_t)
    dec_t = jnp.exp(r_t + la_t)
    lane = lax.broadcasted_iota(jnp.int32, (dk, tm), 1)
    first_half = (lane & CHUNK) == 0
    kt_even = jnp.where(first_half, kt_t, 0.0).astype(BF16)
    kt_odd = jnp.where(first_half, 0.0, kt_t).astype(BF16)

    n_chunks = tm // CHUNK
    for c in range(n_chunks):
        j = (c * CHUNK) // LANES
        kt_src = kt_even if c % 2 == 0 else kt_odd
        for hd in range(GLA_HEADS):
            k_blk = kt_src[hd * hdk:(hd + 1) * hdk, j * LANES:(j + 1) * LANES]
            v_blk = v[j * LANES:(j + 1) * LANES, hd * hdv:(hd + 1) * hdv]
            kv_ref[c * GLA_HEADS + hd] = _dot(k_blk, v_blk)
    ext = jnp.concatenate([hist_ref[...], p_in], axis=0)
    hist_ref[...] = p_in[tm - POOL_HIST:, :]
    gd = pw // len(POOL_WINDOWS)
    pos = (lax.broadcasted_iota(jnp.int32, (tm, gd), 0) + (s_idx * tm + 1)).astype(F32)
    gate_b = jax.nn.sigmoid(proj(b0 + d, d))
    mixed = []
    for g, w in enumerate(POOL_WINDOWS):
        tok = ext[:, g * gd:(g + 1) * gd]
        win, span = tok, 1
        while span < w:
            win = win + pltpu.roll(win, span, 0)
            span *= 2
        pooled = win[POOL_HIST:, :] / jnp.minimum(pos, float(w)) - tok[POOL_HIST:, :]
        mixed.append(_dot(pooled.astype(BF16), w_pool_ref[g]))
    y_pool = (jnp.concatenate(mixed, axis=1) * pool_scale_ref[...]).astype(BF16)

    for hd in range(GLA_HEADS):
        state = state_ref[hd]
        for c in range(n_chunks):
            dcol = dec_t[hd * hdk:(hd + 1) * hdk, c * CHUNK:c * CHUNK + 1]
            state = dcol * state + kv_ref[c * GLA_HEADS + hd]
            q_c = q[c * CHUNK:(c + 1) * CHUNK, hd * hdk:(hd + 1) * hdk]
            oscr_ref[c * CHUNK:(c + 1) * CHUNK, hd * hdv:(hd + 1) * hdv] = _dot(
                q_c, state.astype(BF16))
        state_ref[hd] = state

    hx = xw // XA_HEADS
    scores = [_dot(xq[:, hd * hx:(hd + 1) * hx], kt_ref[hd * hx:(hd + 1) * hx, :])
              for hd in range(XA_HEADS)]
    gate_c = jax.nn.sigmoid(proj(b0 + 2 * d, d))
    y_b = _dot(y_pool, w_up_pool_ref[...])
    gla_g = gla_g_ref[...]
    o_n = jnp.concatenate(
        [_rms(oscr_ref[:, i:i + hdv], gla_g[:, i:i + hdv]) for i in range(0, dv, hdv)], axis=1)
    y_a = _dot((o_n * gated).astype(BF16), w_up_gla_ref[...])
    heads = []
    for hd in range(XA_HEADS):
        s = scores[hd] * (hx ** -0.5)
        e = jnp.exp(s - jnp.max(s, axis=-1, keepdims=True))
        pr = (e / jnp.sum(e, axis=-1, keepdims=True)).astype(BF16)
        heads.append(_dot(pr, v_ref[:, hd * hx:(hd + 1) * hx]))
    y_c = _dot(jnp.concatenate(heads, axis=1).astype(BF16), w_up_x_ref[...])
    merged = (gate_a * y_a + gate_b * y_b + gate_c * y_c).astype(BF16)

    half = tm // 2
    for r in range(0, tm, half):
        y = _dot(merged[r:r + half], w_o_ref[...])
        o_ref[r:r + half, :] = x[r:r + half] + _rms(y, post_g_ref[...])


def _mix(x, pre_g, w_main, w_kft, w_fut, bft, gla_g, w_pool, pool_scale, kt, v, w_up_gla,
         w_up_pool, w_up_x, w_o, post_g, *, tm):
    b, s, d = x.shape
    dk = w_fut.shape[0]
    dv = w_up_gla.shape[0]
    pw = w_up_pool.shape[0]
    xw = w_up_x.shape[0]
    m = v.shape[1]
    assert s % tm == 0 and tm % MXU_DIM == 0
    assert w_main.shape[1] == dk + 2 * dv + pw + xw + N_BRANCH * d
    hdk, hdv = dk // GLA_HEADS, dv // GLA_HEADS
    tile = pl.BlockSpec((None, tm, d), lambda i, j: (i, j, 0))
    vec = _resident((1, d))
    in_specs = [
        tile, vec, _resident(w_main.shape), _resident(w_kft.shape), _resident(w_fut.shape),
        _resident(bft.shape), _resident(gla_g.shape), _resident(w_pool.shape),
        _resident(pool_scale.shape),
        pl.BlockSpec((None, xw, m), lambda i, j: (i, 0, 0)),
        pl.BlockSpec((None, m, xw), lambda i, j: (i, 0, 0)),
        _resident(w_up_gla.shape), _resident(w_up_pool.shape), _resident(w_up_x.shape),
        _resident(w_o.shape), vec,
    ]
    return pl.pallas_call(
        functools.partial(_mix_kernel, tm=tm, d=d, dk=dk, dv=dv, pw=pw, xw=xw),
        out_shape=jax.ShapeDtypeStruct((b, s, d), F32),
        grid=(b, s // tm),
        in_specs=in_specs,
        out_specs=tile,
        scratch_shapes=[
            pltpu.VMEM((GLA_HEADS, hdk, hdv), F32),
            pltpu.VMEM((POOL_HIST, pw), F32),
            pltpu.VMEM((tm, dv), F32),
            pltpu.VMEM((tm // CHUNK * GLA_HEADS, hdk, hdv), F32),
        ],
        compiler_params=pltpu.CompilerParams(
            dimension_semantics=("arbitrary", "arbitrary"), vmem_limit_bytes=VMEM_LIMIT_BYTES),
        name="mix",
    )(x, pre_g, w_main, w_kft, w_fut, bft, gla_g, w_pool, pool_scale, kt, v, w_up_gla,
      w_up_pool, w_up_x, w_o, post_g)


def _pad_to(a, shape):
    return jnp.pad(a, [(0, t - s) for s, t in zip(a.shape, shape)])


def kernel(x, mem, ffn1_pre_g, ffn1_w_in, ffn1_w_out, ffn1_post_g, mix_pre_g, w_in, w_fu, b_f, gla_norm_g, w_pool, pool_scale, mem_norm_g, w_mem_kv, w_up_gla, w_up_pool, w_up_xattn, w_o, mix_post_g, ffn2_pre_g, ffn2_w_in, ffn2_w_out, ffn2_post_g, final_g):
    b, s, d = x.shape
    tm = MIX_TM
    rank, dk = w_fu.shape[1], w_fu.shape[2]
    dv = w_up_gla.shape[1]
    pw = w_up_pool.shape[1]
    xw = w_up_xattn.shape[1]
    for l in range(ffn1_w_in.shape[0]):
        vec = lambda a: a[l].reshape(1, -1)
        x2d = _ffn(x.reshape(b * s, d), vec(ffn1_pre_g), ffn1_w_in[l].astype(BF16),
                   ffn1_w_out[l].astype(BF16), vec(ffn1_post_g), None, tm=FFN_TM)

        w = w_in[l]
        edges = [0, dk, 2 * dk, 2 * dk + dv, 2 * dk + 2 * dv]
        edges += [edges[-1] + rank]
        edges += [edges[-1] + pw]
        edges += [edges[-1] + xw, w.shape[1]]
        w_q, w_k, w_v, w_g, w_f, w_p, w_x, w_b = (
            w[:, lo:hi] for lo, hi in zip(edges[:-1], edges[1:]))
        w_main = jnp.concatenate([w_q, w_v, w_g, w_p, w_x, w_b], axis=1).astype(BF16)
        w_kft = jnp.concatenate([_pad_to(w_f.T, (LANES, d)), w_k.T], axis=0).astype(BF16)
        w_fut = _pad_to(w_fu[l].T, (dk, LANES)).astype(BF16)
        w_kv = w_mem_kv[l]
        kt, v = _memkv(mem, vec(mem_norm_g), w_kv[:, :xw].T.astype(BF16),
                       w_kv[:, xw:].astype(BF16))
        x = _mix(x2d.reshape(b, s, d), vec(mix_pre_g), w_main, w_kft, w_fut,
                 b_f[l].reshape(dk, 1), vec(gla_norm_g), w_pool[l].astype(BF16),
                 vec(pool_scale), kt, v, w_up_gla[l].astype(BF16), w_up_pool[l].astype(BF16),
                 w_up_xattn[l].astype(BF16), w_o[l].astype(BF16), vec(mix_post_g), tm=tm)

        x = _ffn(x.reshape(b * s, d), vec(ffn2_pre_g), ffn2_w_in[l].astype(BF16),
                 ffn2_w_out[l].astype(BF16), vec(ffn2_post_g), vec(final_g), tm=FFN_TM
                 ).reshape(b, s, d)
    return x
```
